```python
import math
import jax, jax.numpy as jnp
from jax import lax
import numpy as np

D_MODEL = 2048
BATCH = 2
SEQ = 16384
DEPTH = 2

CHUNK = 64
N_META = 16
MIX_WIDTH = D_MODEL
LRU_WIDTH = MIX_WIDTH // 2
LRU_BLOCKS = 8
LRU_BLOCK_W = LRU_WIDTH // LRU_BLOCKS
CONV_W = 4
LRU_C = 8.0
ATT_HEADS = 8
ATT_QK_DIM = (MIX_WIDTH - LRU_WIDTH) // (2 * ATT_HEADS)
ATT_V_DIM = 2 * ATT_QK_DIM
ATT_WIDTH = ATT_HEADS * ATT_V_DIM
QK_WIDTH = ATT_HEADS * 2 * ATT_QK_DIM
IN_PROJ_WIDTH = 2 * LRU_WIDTH + 2 * QK_WIDTH + ATT_WIDTH
Q_BLOCK = 128
N_GROUPS = 4
EXPERTS_PER_GROUP = 4
N_EXPERTS = N_GROUPS * EXPERTS_PER_GROUP
TOP_K_INNER = 2
D_EXPERT = 512
RMS_EPS = 1e-6

kernel_name = "hybrid_rglru_diffattn_hmoe_block"


def rmsnorm(x, g):
    xf = x.astype(jnp.float32)
    y = xf * lax.rsqrt(jnp.mean(xf * xf, axis=-1, keepdims=True) + RMS_EPS)
    return (y * g.astype(jnp.float32)).astype(x.dtype)


def chunk_id(pos):
    return jnp.where(pos < N_META, 0, 1 + (pos - N_META) // CHUNK)


def alibi_slopes(n_heads):
    return jnp.exp2(-8.0 * jnp.arange(1, n_heads + 1, dtype=jnp.float32) / n_heads)


def rg_lru_mixer(u, yg, conv_w, conv_b, wa, ba, wx, bx, lru_a_param):
    B, T, W = u.shape
    up = jnp.pad(u, ((0, 0), (CONV_W - 1, 0), (0, 0)))
    xc = conv_b
    for k in range(CONV_W):
        xc = xc + conv_w[k] * up[:, CONV_W - 1 - k: CONV_W - 1 - k + T]
    xh = xc.reshape(B, T, LRU_BLOCKS, LRU_BLOCK_W)
    r = jax.nn.sigmoid(jnp.einsum('btgi,gij->btgj', xh, wa).reshape(B, T, W) + ba)
    i = jax.nn.sigmoid(jnp.einsum('btgi,gij->btgj', xh, wx).reshape(B, T, W) + bx)
    log_a = (-LRU_C * r.astype(jnp.float32)) * jax.nn.softplus(-lru_a_param.astype(jnp.float32))
    a = jnp.exp(log_a)
    b = jnp.sqrt(-jnp.expm1(2.0 * log_a)) * (i * xc).astype(jnp.float32)

    def combine(left, right):
        a_l, b_l = left
        a_r, b_r = right
        return a_l * a_r, a_r * b_l + b_r

    _, h = lax.associative_scan(combine, (a, b), axis=1)
    return h.astype(u.dtype) * jax.nn.gelu(yg)


def diff_attention(q, k, v, lam, lam_init, subln_g):
    B, T, _ = q.shape
    H, Dh, Dv = ATT_HEADS, ATT_QK_DIM, ATT_V_DIM
    q = q.reshape(B, T, H, 2, Dh).transpose(0, 2, 3, 1, 4)
    k = k.reshape(B, T, H, 2, Dh).transpose(0, 2, 3, 1, 4)
    v = v.reshape(B, T, H, Dv).transpose(0, 2, 1, 3)
    n_blk = -(-T // Q_BLOCK)
    t_pad = n_blk * Q_BLOCK
    qp = jnp.pad(q, ((0, 0), (0, 0), (0, 0), (0, t_pad - T), (0, 0)))
    q_blocks = qp.reshape(B, H, 2, n_blk, Q_BLOCK, Dh).transpose(3, 0, 1, 2, 4, 5)
    pos_blocks = jnp.arange(t_pad).reshape(n_blk, Q_BLOCK)
    pos_k = jnp.arange(T)
    cid_k = chunk_id(pos_k)
    slopes = alibi_slopes(H)
    scale = Dh ** -0.5

    def block(args):
        q_blk, pos_q = args
        visible = chunk_id(pos_q)[:, None] >= cid_k[None, :]
        dist = jnp.abs(pos_q[:, None] - pos_k[None, :]).astype(jnp.float32)
        bias = jnp.where(visible[None], -slopes[:, None, None] * dist[None], -jnp.inf)
        s = jnp.einsum('bhmqd,bhmkd->bhmqk', q_blk, k).astype(jnp.float32) * scale + bias[None, :, None]
        p = jax.nn.softmax(s, axis=-1)
        a = (p[:, :, 0] - lam * p[:, :, 1]).astype(v.dtype)
        return jnp.einsum('bhqk,bhkd->bhqd', a, v)

    o = lax.map(block, (q_blocks, pos_blocks))
    o = o.transpose(1, 2, 0, 3, 4).reshape(B, H, t_pad, Dv)[:, :, :T]
    o = rmsnorm(o, subln_g) * (1.0 - lam_init)
    return o.transpose(0, 2, 1, 3).reshape(B, T, H * Dv)


def hier_moe(x, rg_w, rg_b, re_w, re_b, w_gate, w_up, w_down):
    shape = x.shape
    xt = x.reshape(-1, shape[-1])
    n = xt.shape[0]
    lg_g = (xt @ rg_w + rg_b).astype(jnp.float32)
    p_g = jax.nn.softmax(lg_g, axis=-1)
    _, g_idx = lax.top_k(lg_g, 1)
    g_prob = jnp.take_along_axis(p_g, g_idx, axis=1)
    lg_e = (xt @ re_w + re_b).astype(jnp.float32).reshape(n, N_GROUPS, EXPERTS_PER_GROUP)
    lg_in = jnp.take_along_axis(lg_e, g_idx[:, :, None], axis=1)[:, 0]
    top_v, top_i = lax.top_k(lg_in, TOP_K_INNER)
    p_in = jax.nn.softmax(top_v, axis=-1)
    w_in = jnp.sum(jax.nn.one_hot(top_i, EXPERTS_PER_GROUP, dtype=jnp.float32) * p_in[..., None], axis=1)
    gate = (jax.nn.one_hot(g_idx[:, 0], N_GROUPS, dtype=jnp.float32)[:, :, None]
            * w_in[:, None, :]).reshape(n, N_EXPERTS) * g_prob
    gate = gate.astype(xt.dtype)
    y = jnp.zeros_like(xt)
    for e in range(N_EXPERTS):
        he = jax.nn.silu(xt @ w_gate[e]) * (xt @ w_up[e])
        y = y + gate[:, e:e + 1] * (he @ w_down[e])
    return y.reshape(shape)


def setup_inputs(seed: int = 0) -> dict:
    key = jax.random.key(seed)
    ks = jax.random.split(key, 28)
    f32 = jnp.float32

    def nrm(k, shape, scale):
        return jax.random.normal(k, shape, f32) * scale

    rad = jax.random.uniform(ks[10], (DEPTH, LRU_WIDTH), f32, 0.9, 0.999)
    s = rad ** (1.0 / LRU_C)
    lru_a_param = jnp.log(s) - jnp.log1p(-s)
    return {
        "x": nrm(ks[0], (BATCH, SEQ, D_MODEL), 1.0),
        "meta_tokens": nrm(ks[1], (N_META, D_MODEL), 1.0),
        "norm1_g": 1.0 + nrm(ks[2], (DEPTH, D_MODEL), 0.02),
        "w_in": nrm(ks[3], (DEPTH, D_MODEL, IN_PROJ_WIDTH), D_MODEL ** -0.5),
        "conv_w": nrm(ks[4], (DEPTH, CONV_W, LRU_WIDTH), CONV_W ** -0.5),
        "conv_b": nrm(ks[5], (DEPTH, LRU_WIDTH), 0.02),
        "gate_a_w": nrm(ks[6], (DEPTH, LRU_BLOCKS, LRU_BLOCK_W, LRU_BLOCK_W), LRU_BLOCK_W ** -0.5),
        "gate_a_b": nrm(ks[7], (DEPTH, LRU_WIDTH), 0.02),
        "gate_x_w": nrm(ks[8], (DEPTH, LRU_BLOCKS, LRU_BLOCK_W, LRU_BLOCK_W), LRU_BLOCK_W ** -0.5),
        "gate_x_b": nrm(ks[9], (DEPTH, LRU_WIDTH), 0.02),
        "lru_a_param": lru_a_param,
        "lambda_q1": nrm(ks[11], (DEPTH, ATT_QK_DIM), 0.1),
        "lambda_k1": nrm(ks[12], (DEPTH, ATT_QK_DIM), 0.1),
        "lambda_q2": nrm(ks[13], (DEPTH, ATT_QK_DIM), 0.1),
        "lambda_k2": nrm(ks[14], (DEPTH, ATT_QK_DIM), 0.1),
        "subln_g": 1.0 + nrm(ks[15], (DEPTH, ATT_V_DIM), 0.02),
        "w_out": nrm(ks[16], (DEPTH, MIX_WIDTH, D_MODEL), MIX_WIDTH ** -0.5),
        "norm2_g": 1.0 + nrm(ks[17], (DEPTH, D_MODEL), 0.02),
        "router_g_w": nrm(ks[18], (DEPTH, D_MODEL, N_GROUPS), D_MODEL ** -0.5),
        "router_g_b": nrm(ks[19], (DEPTH, N_GROUPS), 0.01),
        "router_e_w": nrm(ks[20], (DEPTH, D_MODEL, N_EXPERTS), D_MODEL ** -0.5),
        "router_e_b": nrm(ks[21], (DEPTH, N_EXPERTS), 0.01),
        "exp_w_gate": nrm(ks[22], (DEPTH, N_EXPERTS, D_MODEL, D_EXPERT), D_MODEL ** -0.5),
        "exp_w_up": nrm(ks[23], (DEPTH, N_EXPERTS, D_MODEL, D_EXPERT), D_MODEL ** -0.5),
        "exp_w_down": nrm(ks[24], (DEPTH, N_EXPERTS, D_EXPERT, D_MODEL), D_EXPERT ** -0.5),
        "final_norm_g": 1.0 + nrm(ks[25], (D_MODEL,), 0.02),
    }


def reference(x, meta_tokens, norm1_g, w_in, conv_w, conv_b, gate_a_w, gate_a_b, gate_x_w, gate_x_b,
              lru_a_param, lambda_q1, lambda_k1, lambda_q2, lambda_k2, subln_g, w_out, norm2_g,
              router_g_w, router_g_b, router_e_w, router_e_b, exp_w_gate, exp_w_up, exp_w_down,
              final_norm_g):
    B = x.shape[0]
    meta = jnp.broadcast_to(meta_tokens[None].astype(x.dtype), (B, N_META, D_MODEL))
    h = jnp.concatenate([meta, x], axis=1)
    splits = [LRU_WIDTH, 2 * LRU_WIDTH, 2 * LRU_WIDTH + QK_WIDTH, 2 * LRU_WIDTH + 2 * QK_WIDTH]
    for l in range(DEPTH):
        lam_init = 0.8 - 0.6 * math.exp(-0.3 * l)
        hn = rmsnorm(h, norm1_g[l])
        proj = hn @ w_in[l]
        u, yg, q, k, v = jnp.split(proj, splits, axis=-1)
        lru_out = rg_lru_mixer(u, yg, conv_w[l], conv_b[l], gate_a_w[l], gate_a_b[l],
                               gate_x_w[l], gate_x_b[l], lru_a_param[l])
        lam = (jnp.exp(jnp.sum(lambda_q1[l].astype(jnp.float32) * lambda_k1[l].astype(jnp.float32)))
               - jnp.exp(jnp.sum(lambda_q2[l].astype(jnp.float32) * lambda_k2[l].astype(jnp.float32)))
               + lam_init)
        att_out = diff_attention(q, k, v, lam, lam_init, subln_g[l])
        h = h + jnp.concatenate([lru_out, att_out], axis=-1) @ w_out[l]
        h = h + hier_moe(rmsnorm(h, norm2_g[l]), router_g_w[l], router_g_b[l], router_e_w[l],
                         router_e_b[l], exp_w_gate[l], exp_w_up[l], exp_w_down[l])
    out = rmsnorm(h, final_norm_g)
    return out[:, N_META:]
```

```python
import functools
import math

import jax
import jax.numpy as jnp
from jax import lax
from jax.experimental import pallas as pl
from jax.experimental.pallas import tpu as pltpu

F32 = jnp.float32
BF16 = jnp.bfloat16

N_META = 16
CHUNK = 64
LRU_WIDTH = 1024
LRU_BLOCKS = 8
LRU_BLOCK_W = LRU_WIDTH // LRU_BLOCKS
CONV_W = 4
LRU_C = 8.0
ATT_HEADS = 8
ATT_QK_DIM = 64
ATT_V_DIM = 128
QK_WIDTH = ATT_HEADS * 2 * ATT_QK_DIM
ATT_WIDTH = ATT_HEADS * ATT_V_DIM
N_GROUPS = 4
EXPERTS_PER_GROUP = 4
N_EXPERTS = N_GROUPS * EXPERTS_PER_GROUP
D_EXPERT = 512
RMS_EPS = 1e-6

LANES = 128
SUBLANES = 8
TQ = 512
ROUTER_COLS = LANES
VMEM_LIMIT = 56 * 1024 * 1024


def _cparams(sem):
    return pltpu.CompilerParams(dimension_semantics=sem, vmem_limit_bytes=VMEM_LIMIT)


def _inproj_kernel(x_ref, g_ref, w_ref, o_ref, xn_ref):
    @pl.when(pl.program_id(1) == 0)
    def _():
        x = x_ref[...]
        ms = jnp.mean(x * x, axis=-1, keepdims=True)
        xn_ref[...] = (x * lax.rsqrt(ms + RMS_EPS) * g_ref[...]).astype(BF16)

    o_ref[...] = jnp.dot(xn_ref[...], w_ref[...], preferred_element_type=F32).astype(o_ref.dtype)


def _inproj(h, g, w, *, tm, tn):
    n, d = h.shape
    width = w.shape[1]
    return pl.pallas_call(
        _inproj_kernel,
        out_shape=jax.ShapeDtypeStruct((n, width), BF16),
        grid=(n // tm, width // tn),
        in_specs=[
            pl.BlockSpec((tm, d), lambda i, j: (i, 0)),
            pl.BlockSpec((1, d), lambda i, j: (0, 0)),
            pl.BlockSpec((d, tn), lambda i, j: (0, j)),
        ],
        out_specs=pl.BlockSpec((tm, tn), lambda i, j: (i, j)),
        scratch_shapes=[pltpu.VMEM((tm, d), BF16)],
        compiler_params=_cparams(("parallel", "arbitrary")),
        name="inproj",
    )(h, g, w)


def _gelu_tanh(x):
    return 0.5 * x * (1.0 + jnp.tanh(math.sqrt(2.0 / math.pi) * (x + 0.044715 * (x * x * x))))


def _lru_kernel(u_ref, yg_ref, cw_ref, cb_ref, wa_ref, ba_ref, wx_ref, bx_ref, ap_ref, o_ref,
                ubuf_ref, carry_ref, a_ref, b_ref):
    s = pl.program_id(1)
    tt, w = u_ref.shape
    ng = tt // SUBLANES

    @pl.when(s == 0)
    def _():
        ubuf_ref[0:SUBLANES, :] = jnp.zeros((SUBLANES, w), F32)
        carry_ref[...] = jnp.zeros_like(carry_ref)

    ubuf_ref[SUBLANES:, :] = u_ref[...].astype(F32)
    xc = cb_ref[...]
    for k in range(CONV_W):
        xc = xc + cw_ref[k:k + 1, :] * ubuf_ref[pl.ds(SUBLANES - k, tt), :]

    xcb = xc.astype(BF16)
    ra, ix = [], []
    for g in range(LRU_BLOCKS):
        xg = xcb[:, g * LRU_BLOCK_W:(g + 1) * LRU_BLOCK_W]
        ra.append(jnp.dot(xg, wa_ref[g], preferred_element_type=F32))
        ix.append(jnp.dot(xg, wx_ref[g], preferred_element_type=F32))
    r = jax.nn.sigmoid(jnp.concatenate(ra, axis=1) + ba_ref[...])
    i = jax.nn.sigmoid(jnp.concatenate(ix, axis=1) + bx_ref[...])

    nl = -ap_ref[...]
    softplus = jnp.maximum(nl, 0.0) + jnp.log1p(jnp.exp(-jnp.abs(nl)))
    log_a = (-LRU_C * r) * softplus
    a = jnp.exp(log_a)
    b = jnp.sqrt(-jnp.tanh(log_a) * (a * a + 1.0)) * (i * xc)

    a3 = a.reshape(ng, SUBLANES, w)
    b3 = b.reshape(ng, SUBLANES, w)
    row = lax.broadcasted_iota(jnp.int32, (ng, SUBLANES, w), 1)
    for d in (1, 2, 4):
        keep = row >= d
        a_sh = jnp.where(keep, pltpu.roll(a3, d, axis=1), 1.0)
        b_sh = jnp.where(keep, pltpu.roll(b3, d, axis=1), 0.0)
        b3 = a3 * b_sh + b3
        a3 = a3 * a_sh
    a_ref[...] = a3
    b_ref[...] = b3

    def body(g, carry):
        h_prev, h_meta = carry
        h8 = a_ref[g] * h_prev + b_ref[g]
        b_ref[g] = h8
        h_last = h8[SUBLANES - 1:SUBLANES, :]
        h_meta = jnp.where(g == N_META // SUBLANES - 1, h_last, h_meta)
        return h_last, h_meta

    h0 = carry_ref[...]
    h_last, h_meta = lax.fori_loop(0, ng, body, (h0, h0))
    is_meta = s == 0
    carry_ref[...] = jnp.where(is_meta, h_meta, h_last)
    ubuf_ref[0:SUBLANES, :] = jnp.where(is_meta, ubuf_ref[N_META:N_META + SUBLANES, :],
                                        ubuf_ref[tt:tt + SUBLANES, :])

    h = b_ref[...].reshape(tt, w)
    o_ref[...] = (h * _gelu_tanh(yg_ref[...].astype(F32))).astype(o_ref.dtype)


def _lru(proj, cw, cb, wa, ba, wx, bx, ap, *, batch, nt):
    n = proj.shape[0]
    w = LRU_WIDTH
    ntp = nt + 1

    def row_block(b, s):
        return b * ntp + (s + nt) % ntp

    full2 = lambda b, s: (0, 0)
    full3 = lambda b, s: (0, 0, 0)
    return pl.pallas_call(
        _lru_kernel,
        out_shape=jax.ShapeDtypeStruct((n, w), BF16),
        grid=(batch, ntp),
        in_specs=[
            pl.BlockSpec((TQ, w), lambda b, s: (row_block(b, s), 0)),
            pl.BlockSpec((TQ, w), lambda b, s: (row_block(b, s), 1)),
            pl.BlockSpec((CONV_W, w), full2),
            pl.BlockSpec((1, w), full2),
            pl.BlockSpec((LRU_BLOCKS, LRU_BLOCK_W, LRU_BLOCK_W), full3),
            pl.BlockSpec((1, w), full2),
            pl.BlockSpec((LRU_BLOCKS, LRU_BLOCK_W, LRU_BLOCK_W), full3),
            pl.BlockSpec((1, w), full2),
            pl.BlockSpec((1, w), full2),
        ],
        out_specs=pl.BlockSpec((TQ, w), lambda b, s: (row_block(b, s), 0)),
        scratch_shapes=[
            pltpu.VMEM((TQ + SUBLANES, w), F32),
            pltpu.VMEM((1, w), F32),
            pltpu.VMEM((TQ // SUBLANES, SUBLANES, w), F32),
            pltpu.VMEM((TQ // SUBLANES, SUBLANES, w), F32),
        ],
        compiler_params=_cparams(("parallel", "arbitrary")),
        name="rglru",
    )(proj, proj, cw, cb, wa, ba, wx, bx, ap)


def _attn_kernel(slopes_ref, lam_ref, q_ref, k_ref, vt_ref, g_ref, o_ref, m_ref, l_ref, acc_ref,
                 *, nt, lam_init):
    h = pl.program_id(1)
    qi = pl.program_id(2)
    tq = TQ
    tk = TQ
    slope = slopes_ref[h]
    is_meta = qi == nt
    nt_dims = (((1,), (1,)), ((), ()))

    q = q_ref[...]
    lane = lax.broadcasted_iota(jnp.int32, q.shape, 1)
    zero = jnp.zeros_like(q)
    qm = (jnp.where(lane < ATT_QK_DIM, q, zero), jnp.where(lane >= ATT_QK_DIM, q, zero))

    def step(mi, s_rel, off, vt, first):
        t_max = jnp.max(s_rel, axis=0, keepdims=True) + off
        if first:
            m_new = t_max
        else:
            m_old = m_ref[mi]
            m_new = jnp.maximum(m_old, t_max)
        p = jnp.exp(s_rel - (m_new - off))
        pv = jnp.dot(vt, p.astype(BF16), preferred_element_type=F32)
        ps = jnp.sum(p, axis=0, keepdims=True)
        if first:
            l_ref[mi] = ps
            acc_ref[mi] = pv
        else:
            alpha = jnp.exp(m_old - m_new)
            l_ref[mi] = alpha * l_ref[mi] + ps
            acc_ref[mi] = alpha * acc_ref[mi] + pv
        m_ref[mi] = m_new

    mk = LANES
    k_meta = k_ref[pl.ds(nt * tk, mk), :]
    vt_meta = vt_ref[nt][:, 0:mk]
    r_m = lax.broadcasted_iota(jnp.int32, (mk, tq), 0)
    c_m = lax.broadcasted_iota(jnp.int32, (mk, tq), 1)
    pq_m = jnp.where(is_meta, c_m, N_META + qi * tq + c_m)
    bias_m = slope * r_m.astype(F32) - (2.0 * slope) * jnp.maximum(r_m - pq_m, 0).astype(F32)
    zero_off = jnp.zeros((1, tq), F32)
    for mi in range(2):
        s = lax.dot_general(k_meta, qm[mi], nt_dims, preferred_element_type=F32)
        s_rel = jnp.where(r_m < N_META, s + bias_m, -jnp.inf)
        step(mi, s_rel, zero_off, vt_meta, True)

    row_bias = slope * lax.broadcasted_iota(jnp.int32, (tk, tq), 0).astype(F32)
    c_row = lax.broadcasted_iota(jnp.int32, (1, tq), 1)

    def body(j, carry):
        kt = k_ref[pl.ds(pl.multiple_of(j * tk, tk), tk), :]
        vt = vt_ref[j]
        off = slope * (N_META + j * tk + 0 * c_row).astype(F32)
        for mi in range(2):
            s = lax.dot_general(kt, qm[mi], nt_dims, preferred_element_type=F32)
            step(mi, s + row_bias, off, vt, False)
        return carry

    lax.fori_loop(0, jnp.where(is_meta, 0, qi), body, 0)

    @pl.when(jnp.logical_not(is_meta))
    def _():
        kt = k_ref[pl.ds(pl.multiple_of(qi * tk, tk), tk), :]
        vt = vt_ref[qi]
        r = lax.broadcasted_iota(jnp.int32, (tk, tq), 0)
        c = lax.broadcasted_iota(jnp.int32, (tk, tq), 1)
        visible = (r // CHUNK) <= (c // CHUNK)
        bias = row_bias - (2.0 * slope) * jnp.maximum(r - c, 0).astype(F32)
        off = slope * (N_META + qi * tk + 0 * c_row).astype(F32)
        for mi in range(2):
            s = lax.dot_general(kt, qm[mi], nt_dims, preferred_element_type=F32)
            step(mi, jnp.where(visible, s + bias, -jnp.inf), off, vt, False)

    lam = (jnp.exp(jnp.sum(lam_ref[0:1, :] * lam_ref[1:2, :], axis=1, keepdims=True))
           - jnp.exp(jnp.sum(lam_ref[2:3, :] * lam_ref[3:4, :], axis=1, keepdims=True)) + lam_init)
    o = acc_ref[0] / l_ref[0] - lam * (acc_ref[1] / l_ref[1])
    ms = jnp.mean(o * o, axis=0, keepdims=True)
    on = (o * lax.rsqrt(ms + RMS_EPS) * g_ref[...]) * (1.0 - lam_init)
    o_ref[...] = on.T.astype(o_ref.dtype)


def _attention(proj, vt, slopes, lam_vecs, subln_g, *, batch, nt, lam_init):
    n = proj.shape[0]
    ntp = nt + 1
    tp = ntp * TQ
    q_col0 = 2 * LRU_WIDTH // LANES
    k_col0 = (2 * LRU_WIDTH + QK_WIDTH) // LANES
    kern = functools.partial(_attn_kernel, nt=nt, lam_init=lam_init)
    return pl.pallas_call(
        kern,
        out_shape=jax.ShapeDtypeStruct((n, ATT_WIDTH), BF16),
        grid=(batch, ATT_HEADS, ntp),
        in_specs=[
            pl.BlockSpec(memory_space=pltpu.SMEM),
            pl.BlockSpec((4, ATT_QK_DIM), lambda b, h, qi: (0, 0)),
            pl.BlockSpec((TQ, 2 * ATT_QK_DIM), lambda b, h, qi: (b * ntp + qi, q_col0 + h)),
            pl.BlockSpec((tp, 2 * ATT_QK_DIM), lambda b, h, qi: (b, k_col0 + h)),
            pl.BlockSpec((None, None, ntp, ATT_V_DIM, TQ), lambda b, h, qi: (b, h, 0, 0, 0)),
            pl.BlockSpec((ATT_V_DIM, 1), lambda b, h, qi: (0, 0)),
        ],
        out_specs=pl.BlockSpec((TQ, ATT_V_DIM), lambda b, h, qi: (b * ntp + qi, h)),
        scratch_shapes=[
            pltpu.VMEM((2, 1, TQ), F32),
            pltpu.VMEM((2, 1, TQ), F32),
            pltpu.VMEM((2, ATT_V_DIM, TQ), F32),
        ],
        compiler_params=_cparams(("parallel", "parallel", "arbitrary")),
        name="diffattn",
    )(slopes, lam_vecs, proj, proj, vt, subln_g)


def _outproj_kernel(h_ref, lru_ref, att_ref, wt_ref, wb_ref, g_ref, rw_ref, rb_ref,
                    hn_ref, xn_ref, gate_ref):
    h = (h_ref[...] + jnp.dot(lru_ref[...], wt_ref[...], preferred_element_type=F32)
         + jnp.dot(att_ref[...], wb_ref[...], preferred_element_type=F32))
    hn_ref[...] = h
    ms = jnp.mean(h * h, axis=-1, keepdims=True)
    xn = h * lax.rsqrt(ms + RMS_EPS) * g_ref[...]
    xn_ref[...] = xn.astype(xn_ref.dtype)

    logits = jnp.dot(xn, rw_ref[...], precision=lax.Precision.HIGHEST,
                     preferred_element_type=F32) + rb_ref[...]
    col = lax.broadcasted_iota(jnp.int32, logits.shape, 1)
    colf = col.astype(F32)
    big = float(ROUTER_COLS)
    neg = -jnp.inf
    is_g = col < N_GROUPS
    gl = jnp.where(is_g, logits, neg)
    gmax = jnp.max(gl, axis=1, keepdims=True)
    gsum = jnp.sum(jnp.where(is_g, jnp.exp(gl - gmax), 0.0), axis=1, keepdims=True)
    g_prob = 1.0 / gsum
    g_idx = jnp.min(jnp.where(gl == gmax, colf, big), axis=1, keepdims=True)
    ecol = col - N_GROUPS
    in_grp = (ecol >= 0) & (ecol < N_EXPERTS) & ((ecol // EXPERTS_PER_GROUP).astype(F32) == g_idx)
    el = jnp.where(in_grp, logits, neg)
    v1 = jnp.max(el, axis=1, keepdims=True)
    i1 = jnp.min(jnp.where(el == v1, colf, big), axis=1, keepdims=True)
    el2 = jnp.where(colf == i1, neg, el)
    v2 = jnp.max(el2, axis=1, keepdims=True)
    i2 = jnp.min(jnp.where(el2 == v2, colf, big), axis=1, keepdims=True)
    e2 = jnp.exp(v2 - v1)
    den = 1.0 + e2
    w_in = jnp.where(colf == i1, 1.0 / den, 0.0) + jnp.where(colf == i2, e2 / den, 0.0)
    gate_ref[...] = w_in * g_prob


def _outproj(h, lru, att, wt, wb, g, rw, rb, *, tm):
    n, d = h.shape
    c2 = lambda i: (0, 0)
    return pl.pallas_call(
        _outproj_kernel,
        out_shape=(jax.ShapeDtypeStruct((n, d), F32),
                   jax.ShapeDtypeStruct((n, d), BF16),
                   jax.ShapeDtypeStruct((n, ROUTER_COLS), F32)),
        grid=(n // tm,),
        in_specs=[
            pl.BlockSpec((tm, d), lambda i: (i, 0)),
            pl.BlockSpec((tm, LRU_WIDTH), lambda i: (i, 0)),
            pl.BlockSpec((tm, ATT_WIDTH), lambda i: (i, 0)),
            pl.BlockSpec((LRU_WIDTH, d), c2),
            pl.BlockSpec((ATT_WIDTH, d), c2),
            pl.BlockSpec((1, d), c2),
            pl.BlockSpec((d, ROUTER_COLS), c2),
            pl.BlockSpec((1, ROUTER_COLS), c2),
        ],
        out_specs=(pl.BlockSpec((tm, d), lambda i: (i, 0)),
                   pl.BlockSpec((tm, d), lambda i: (i, 0)),
                   pl.BlockSpec((tm, ROUTER_COLS), lambda i: (i, 0))),
        compiler_params=_cparams(("parallel",)),
        name="outproj_router",
    )(h, lru, att, wt, wb, g, rw, rb)


def _moe_kernel(x_ref, gate_ref, h_ref, wg_ref, wu_ref, wd_ref, o_ref, acc_ref):
    e = pl.program_id(1)

    @pl.when(e == 0)
    def _():
        acc_ref[...] = jnp.zeros_like(acc_ref)

    x = x_ref[...]
    hg = jnp.dot(x, wg_ref[...], preferred_element_type=F32)
    hu = jnp.dot(x, wu_ref[...], preferred_element_type=F32)
    he = (hg * jax.nn.sigmoid(hg)) * hu
    gate = gate_ref[...]
    col = lax.broadcasted_iota(jnp.int32, gate.shape, 1)
    ge = jnp.sum(jnp.where(col == e + N_GROUPS, gate, 0.0), axis=1, keepdims=True)
    y = jnp.dot(he.astype(BF16), wd_ref[...], preferred_element_type=F32)
    acc_ref[...] += ge * y

    @pl.when(e == N_EXPERTS - 1)
    def _():
        o_ref[...] = h_ref[...] + acc_ref[...]


def _moe(xn, gate, h, wg, wu, wd, *, tm):
    n, d = h.shape
    return pl.pallas_call(
        _moe_kernel,
        out_shape=jax.ShapeDtypeStruct((n, d), F32),
        grid=(n // tm, N_EXPERTS),
        in_specs=[
            pl.BlockSpec((tm, d), lambda i, e: (i, 0)),
            pl.BlockSpec((tm, ROUTER_COLS), lambda i, e: (i, 0)),
            pl.BlockSpec((tm, d), lambda i, e: (i, 0)),
            pl.BlockSpec((None, d, D_EXPERT), lambda i, e: (e, 0, 0)),
            pl.BlockSpec((None, d, D_EXPERT), lambda i, e: (e, 0, 0)),
            pl.BlockSpec((None, D_EXPERT, d), lambda i, e: (e, 0, 0)),
        ],
        out_specs=pl.BlockSpec((tm, d), lambda i, e: (i, 0)),
        scratch_shapes=[pltpu.VMEM((tm, d), F32)],
        compiler_params=_cparams(("parallel", "arbitrary")),
        name="moe_dense",
    )(xn, gate, h, wg, wu, wd)


def _final_norm_kernel(h_ref, g_ref, o_ref):
    x = h_ref[...]
    ms = jnp.mean(x * x, axis=-1, keepdims=True)
    o_ref[...] = x * lax.rsqrt(ms + RMS_EPS) * g_ref[...]


def _final_norm(h, g, *, batch, nt):
    d = h.shape[1]
    ntp = nt + 1
    return pl.pallas_call(
        _final_norm_kernel,
        out_shape=jax.ShapeDtypeStruct((batch, nt * TQ, d), F32),
        grid=(batch, nt),
        in_specs=[
            pl.BlockSpec((TQ, d), lambda b, i: (b * ntp + i, 0)),
            pl.BlockSpec((1, d), lambda b, i: (0, 0)),
        ],
        out_specs=pl.BlockSpec((None, TQ, d), lambda b, i: (b, i, 0)),
        compiler_params=_cparams(("parallel", "parallel")),
        name="final_norm",
    )(h, g)


def _row_tile(n, want):
    t = want
    while n % t:
        t //= 2
    return t


def kernel(x, meta_tokens, norm1_g, w_in, conv_w, conv_b, gate_a_w, gate_a_b, gate_x_w, gate_x_b, lru_a_param, lambda_q1, lambda_k1, lambda_q2, lambda_k2, subln_g, w_out, norm2_g, router_g_w, router_g_b, router_e_w, router_e_b, exp_w_gate, exp_w_up, exp_w_down, final_norm_g):
    batch, t, d = x.shape
    depth = w_in.shape[0]
    assert t % TQ == 0 and N_META <= LANES
    nt = t // TQ
    ntp = nt + 1
    tp = ntp * TQ
    n = batch * tp

    meta = jnp.broadcast_to(meta_tokens[None].astype(x.dtype), (batch, N_META, d))
    pad = jnp.zeros((batch, TQ - N_META, d), x.dtype)
    h = jnp.concatenate([x, meta, pad], axis=1).reshape(n, d)

    q_lo, q_hi = 2 * LRU_WIDTH, 2 * LRU_WIDTH + QK_WIDTH
    colsel = (jnp.arange(w_in.shape[2]) >= q_lo) & (jnp.arange(w_in.shape[2]) < q_hi)
    col_scale = jnp.where(colsel, ATT_QK_DIM ** -0.5, 1.0).astype(F32)
    slopes = jnp.exp2(-8.0 * jnp.arange(1, ATT_HEADS + 1, dtype=F32) / ATT_HEADS)
    rpad = jnp.zeros((d, ROUTER_COLS - N_GROUPS - N_EXPERTS), F32)
    bpad = jnp.zeros((ROUTER_COLS - N_GROUPS - N_EXPERTS,), F32)

    tm_proj = _row_tile(n, 1024)
    tm_tok = _row_tile(n, 512)

    for l in range(depth):
        lam_init = 0.8 - 0.6 * math.exp(-0.3 * l)
        w_in_l = (w_in[l] * col_scale).astype(BF16)
        proj = _inproj(h, norm1_g[l][None], w_in_l, tm=tm_proj, tn=512)

        lru = _lru(proj, conv_w[l], conv_b[l][None], gate_a_w[l].astype(BF16), gate_a_b[l][None],
                   gate_x_w[l].astype(BF16), gate_x_b[l][None], lru_a_param[l][None],
                   batch=batch, nt=nt)

        v = proj[:, q_hi + QK_WIDTH:]
        vt = v.reshape(batch, ntp, TQ, ATT_HEADS, ATT_V_DIM).transpose(0, 3, 1, 4, 2)
        lam_vecs = jnp.stack([lambda_q1[l], lambda_k1[l], lambda_q2[l], lambda_k2[l]]).astype(F32)
        att = _attention(proj, vt, slopes, lam_vecs, subln_g[l][:, None].astype(F32),
                         batch=batch, nt=nt, lam_init=lam_init)

        w_out_l = w_out[l].astype(BF16)
        rw = jnp.concatenate([router_g_w[l], router_e_w[l], rpad], axis=1)
        rb = jnp.concatenate([router_g_b[l], router_e_b[l], bpad])[None]
        h, xn, gate = _outproj(h, lru, att, w_out_l[:LRU_WIDTH], w_out_l[LRU_WIDTH:], norm2_g[l][None],
                               rw, rb, tm=tm_tok)

        h = _moe(xn, gate, h, exp_w_gate[l].astype(BF16), exp_w_up[l].astype(BF16),
                 exp_w_down[l].astype(BF16), tm=tm_tok)

    return _final_norm(h, final_norm_g[None], batch=batch, nt=nt)
```

```python
import functools
import math

import jax
import jax.numpy as jnp
from jax import lax
from jax.experimental import pallas as pl
from jax.experimental.pallas import tpu as pltpu

F32 = jnp.float32
BF16 = jnp.bfloat16

N_META = 16
CHUNK = 64
LRU_WIDTH = 1024
LRU_BLOCKS = 8
LRU_BLOCK_W = LRU_WIDTH // LRU_BLOCKS
CONV_W = 4
LRU_C = 8.0
ATT_HEADS = 8
ATT_QK_DIM = 64
ATT_V_DIM = 128
QK_WIDTH = ATT_HEADS * 2 * ATT_QK_DIM
ATT_WIDTH = ATT_HEADS * ATT_V_DIM
N_GROUPS = 4
EXPERTS_PER_GROUP = 4
N_EXPERTS = N_GROUPS * EXPERTS_PER_GROUP
D_EXPERT = 512
RMS_EPS = 1e-6
LOG2E = 1.4426950408889634

LANES = 128
SUBLANES = 8
TQ = 512
ROUTER_COLS = LANES
VMEM_LIMIT = 56 * 1024 * 1024
VT_ROWS = 144
EXP_ROWS = 64


def _cparams(sem):
    return pltpu.CompilerParams(dimension_semantics=sem, vmem_limit_bytes=VMEM_LIMIT)


def _inproj_kernel(x_ref, g_ref, w_ref, o_ref, xn_ref):
    @pl.when(pl.program_id(1) == 0)
    def _():
        x = x_ref[...]
        ms = jnp.mean(x * x, axis=-1, keepdims=True)
        xn_ref[...] = (x * lax.rsqrt(ms + RMS_EPS) * g_ref[...]).astype(BF16)

    o_ref[...] = jnp.dot(xn_ref[...], w_ref[...], preferred_element_type=F32).astype(o_ref.dtype)


def _inproj(h, g, w, *, tm, tn):
    n, d = h.shape
    width = w.shape[1]
    return pl.pallas_call(
        _inproj_kernel,
        out_shape=jax.ShapeDtypeStruct((n, width), BF16),
        grid=(n // tm, width // tn),
        in_specs=[
            pl.BlockSpec((tm, d), lambda i, j: (i, 0)),
            pl.BlockSpec((1, d), lambda i, j: (0, 0)),
            pl.BlockSpec((d, tn), lambda i, j: (0, j)),
        ],
        out_specs=pl.BlockSpec((tm, tn), lambda i, j: (i, j)),
        scratch_shapes=[pltpu.VMEM((tm, d), BF16)],
        compiler_params=_cparams(("parallel", "arbitrary")),
        name="inproj",
    )(h, g, w)


def _gelu_tanh(x):
    return 0.5 * x * (1.0 + jnp.tanh(math.sqrt(2.0 / math.pi) * (x + 0.044715 * (x * x * x))))


def _lru_kernel(u_ref, yg_ref, cw_ref, cb_ref, wa_ref, ba_ref, wx_ref, bx_ref, ap_ref, o_ref,
                ubuf_ref, carry_ref, a_ref, b_ref):
    s = pl.program_id(1)
    tt, w = u_ref.shape
    ng = tt // SUBLANES

    @pl.when(s == 0)
    def _():
        ubuf_ref[0:SUBLANES, :] = jnp.zeros((SUBLANES, w), F32)
        carry_ref[...] = jnp.zeros_like(carry_ref)

    ubuf_ref[SUBLANES:, :] = u_ref[...].astype(F32)
    xc = cb_ref[...]
    for k in range(CONV_W):
        xc = xc + cw_ref[k:k + 1, :] * ubuf_ref[pl.ds(SUBLANES - k, tt), :]

    xcb = xc.astype(BF16)
    ra, ix = [], []
    for g in range(LRU_BLOCKS):
        xg = xcb[:, g * LRU_BLOCK_W:(g + 1) * LRU_BLOCK_W]
        ra.append(jnp.dot(xg, wa_ref[g], preferred_element_type=F32))
        ix.append(jnp.dot(xg, wx_ref[g], preferred_element_type=F32))
    r = jax.nn.sigmoid(jnp.concatenate(ra, axis=1) + ba_ref[...])
    i = jax.nn.sigmoid(jnp.concatenate(ix, axis=1) + bx_ref[...])

    nl = -ap_ref[...]
    softplus = jnp.maximum(nl, 0.0) + jnp.log1p(jnp.exp(-jnp.abs(nl)))
    log_a = (-LRU_C * r) * softplus
    a = jnp.exp(log_a)
    b = jnp.sqrt(-jnp.tanh(log_a) * (a * a + 1.0)) * (i * xc)

    a3 = a.reshape(ng, SUBLANES, w)
    b3 = b.reshape(ng, SUBLANES, w)
    row = lax.broadcasted_iota(jnp.int32, (ng, SUBLANES, w), 1)
    for d in (1, 2, 4):
        keep = row >= d
        a_sh = jnp.where(keep, pltpu.roll(a3, d, axis=1), 1.0)
        b_sh = jnp.where(keep, pltpu.roll(b3, d, axis=1), 0.0)
        b3 = a3 * b_sh + b3
        a3 = a3 * a_sh
    a_ref[...] = a3
    b_ref[...] = b3

    def body(g, carry):
        h_prev, h_meta = carry
        h8 = a_ref[g] * h_prev + b_ref[g]
        b_ref[g] = h8
        h_last = h8[SUBLANES - 1:SUBLANES, :]
        h_meta = jnp.where(g == N_META // SUBLANES - 1, h_last, h_meta)
        return h_last, h_meta

    h0 = carry_ref[...]
    h_last, h_meta = lax.fori_loop(0, ng, body, (h0, h0))
    is_meta = s == 0
    carry_ref[...] = jnp.where(is_meta, h_meta, h_last)
    ubuf_ref[0:SUBLANES, :] = jnp.where(is_meta, ubuf_ref[N_META:N_META + SUBLANES, :],
                                        ubuf_ref[tt:tt + SUBLANES, :])

    h = b_ref[...].reshape(tt, w)
    o_ref[...] = (h * _gelu_tanh(yg_ref[...].astype(F32))).astype(o_ref.dtype)


def _lru(proj, cw, cb, wa, ba, wx, bx, ap, *, batch, nt):
    n = proj.shape[0]
    w = LRU_WIDTH
    ntp = nt + 1

    def row_block(b, s):
        return b * ntp + (s + nt) % ntp

    full2 = lambda b, s: (0, 0)
    full3 = lambda b, s: (0, 0, 0)
    return pl.pallas_call(
        _lru_kernel,
        out_shape=jax.ShapeDtypeStruct((n, w), BF16),
        grid=(batch, ntp),
        in_specs=[
            pl.BlockSpec((TQ, w), lambda b, s: (row_block(b, s), 0)),
            pl.BlockSpec((TQ, w), lambda b, s: (row_block(b, s), 1)),
            pl.BlockSpec((CONV_W, w), full2),
            pl.BlockSpec((1, w), full2),
            pl.BlockSpec((LRU_BLOCKS, LRU_BLOCK_W, LRU_BLOCK_W), full3),
            pl.BlockSpec((1, w), full2),
            pl.BlockSpec((LRU_BLOCKS, LRU_BLOCK_W, LRU_BLOCK_W), full3),
            pl.BlockSpec((1, w), full2),
            pl.BlockSpec((1, w), full2),
        ],
        out_specs=pl.BlockSpec((TQ, w), lambda b, s: (row_block(b, s), 0)),
        scratch_shapes=[
            pltpu.VMEM((TQ + SUBLANES, w), F32),
            pltpu.VMEM((1, w), F32),
            pltpu.VMEM((TQ // SUBLANES, SUBLANES, w), F32),
            pltpu.VMEM((TQ // SUBLANES, SUBLANES, w), F32),
        ],
        compiler_params=_cparams(("parallel", "arbitrary")),
        name="rglru",
    )(proj, proj, cw, cb, wa, ba, wx, bx, ap)


def _attn_kernel(slopes_ref, lam_ref, q_ref, k_ref, vt_ref, g_ref, o_ref,
                 m_ref, acc_ref, s_ref, p_ref, bias_ref, tmax_ref, *, nt, lam_init):
    h = pl.program_id(1)
    qi = pl.program_id(2)
    tq = TQ
    tk = TQ
    slope = slopes_ref[h]
    is_meta = qi == nt
    nt_dims = (((1,), (1,)), ((), ()))

    @pl.when(qi == 0)
    def _():
        r = lax.broadcasted_iota(jnp.int32, (tk, tq), 0)
        c = lax.broadcasted_iota(jnp.int32, (tk, tq), 1)
        row_bias = slope * r.astype(F32)
        bias_ref[0] = row_bias
        diag = row_bias - (2.0 * slope) * jnp.maximum(r - c, 0).astype(F32)
        bias_ref[1] = jnp.where((r // CHUNK) <= (c // CHUNK), diag, -jnp.inf)

    q = q_ref[...]
    lane = lax.broadcasted_iota(jnp.int32, q.shape, 1)
    zero = jnp.zeros_like(q)
    qm = (jnp.where(lane < ATT_QK_DIM, q, zero), jnp.where(lane >= ATT_QK_DIM, q, zero))
    c_row = lax.broadcasted_iota(jnp.int32, (1, tq), 1)

    mk = LANES
    k_meta = k_ref[pl.ds(nt * tk, mk), :]
    vt_meta = vt_ref[nt][:, 0:mk]
    r_m = lax.broadcasted_iota(jnp.int32, (mk, tq), 0)
    c_m = lax.broadcasted_iota(jnp.int32, (mk, tq), 1)
    pq_m = jnp.where(is_meta, c_m, N_META + qi * tq + c_m)
    bias_m = slope * r_m.astype(F32) - (2.0 * slope) * jnp.maximum(r_m - pq_m, 0).astype(F32)
    for mi in range(2):
        s = lax.dot_general(k_meta, qm[mi], nt_dims, preferred_element_type=F32)
        s = jnp.where(r_m < N_META, s + bias_m, -jnp.inf)
        m_new = jnp.max(s, axis=0, keepdims=True)
        p = jnp.exp2(s - m_new).astype(BF16)
        acc_ref[mi] = jnp.dot(vt_meta, p, preferred_element_type=F32)
        m_ref[mi] = m_new

    def scores(j, slot):
        kt = k_ref[pl.ds(pl.multiple_of(j * tk, tk), tk), :]
        bias = bias_ref[(j == qi).astype(jnp.int32)]
        for mi in range(2):
            s = lax.dot_general(kt, qm[mi], nt_dims, preferred_element_type=F32) + bias
            s_ref[slot, mi] = s
            m8 = jnp.max(s.reshape(tk // SUBLANES, SUBLANES, tq), axis=0)
            tmax_ref[slot, mi] = jnp.max(m8, axis=0, keepdims=True)

    def softmax_pv(j, slot):
        vt = vt_ref[j]
        off = slope * (N_META + j * tk + 0 * c_row).astype(F32)
        for mi in range(2):
            m_old = m_ref[mi]
            m_new = jnp.maximum(m_old, tmax_ref[slot, mi] + off)
            m_rel = m_new - off
            for c in range(tk // EXP_ROWS):
                rs = slice(c * EXP_ROWS, (c + 1) * EXP_ROWS)
                p_ref[mi, rs, :] = jnp.exp2(s_ref[slot, mi, rs, :] - m_rel).astype(BF16)
            pv = jnp.dot(vt, p_ref[mi], preferred_element_type=F32)
            acc_ref[mi] = jnp.exp2(m_old - m_new) * acc_ref[mi] + pv
            m_ref[mi] = m_new

    @pl.when(jnp.logical_not(is_meta))
    def _():
        scores(0, 0)

        def body(jj, carry):
            j = 2 * jj
            scores(j + 1, 1)
            softmax_pv(j, 0)
            scores(j + 2, 0)
            softmax_pv(j + 1, 1)
            return carry

        lax.fori_loop(0, qi // 2, body, 0)
        odd = lax.rem(qi, 2) == 1

        @pl.when(odd)
        def _():
            scores(qi, 1)
            softmax_pv(qi - 1, 0)
            softmax_pv(qi, 1)

        @pl.when(jnp.logical_not(odd))
        def _():
            softmax_pv(qi, 0)

    lam =(jnp.exp(jnp.sum(lam_ref[0:1, :] * lam_ref[1:2, :], axis=1, keepdims=True))
           - jnp.exp(jnp.sum(lam_ref[2:3, :] * lam_ref[3:4, :], axis=1, keepdims=True)) + lam_init)
    dv = ATT_V_DIM
    o = (acc_ref[0, 0:dv, :] / acc_ref[0, dv:dv + 1, :]
         - lam * (acc_ref[1, 0:dv, :] / acc_ref[1, dv:dv + 1, :]))
    ms = jnp.mean(o * o, axis=0, keepdims=True)
    on = (o * lax.rsqrt(ms + RMS_EPS) * g_ref[...]) * (1.0 - lam_init)
    o_ref[...] = on.T.astype(o_ref.dtype)


def _attention(proj, vt, slopes, lam_vecs, subln_g, *, batch, nt, lam_init):
    n = proj.shape[0]
    ntp = nt + 1
    tp = ntp * TQ
    q_col0 = 2 * LRU_WIDTH // LANES
    k_col0 = (2 * LRU_WIDTH + QK_WIDTH) // LANES
    kern = functools.partial(_attn_kernel, nt=nt, lam_init=lam_init)
    return pl.pallas_call(
        kern,
        out_shape=jax.ShapeDtypeStruct((n, ATT_WIDTH), BF16),
        grid=(batch, ATT_HEADS, ntp),
        in_specs=[
            pl.BlockSpec(memory_space=pltpu.SMEM),
            pl.BlockSpec((4, ATT_QK_DIM), lambda b, h, qi: (0, 0)),
            pl.BlockSpec((TQ, 2 * ATT_QK_DIM), lambda b, h, qi: (b * ntp + qi, q_col0 + h)),
            pl.BlockSpec((tp, 2 * ATT_QK_DIM), lambda b, h, qi: (b, k_col0 + h)),
            pl.BlockSpec((None, None, ntp, VT_ROWS, TQ), lambda b, h, qi: (b, h, 0, 0, 0)),
            pl.BlockSpec((ATT_V_DIM, 1), lambda b, h, qi: (0, 0)),
        ],
        out_specs=pl.BlockSpec((TQ, ATT_V_DIM), lambda b, h, qi: (b * ntp + qi, h)),
        scratch_shapes=[
            pltpu.VMEM((2, 1, TQ), F32),
            pltpu.VMEM((2, VT_ROWS, TQ), F32),
            pltpu.VMEM((2, 2, TQ, TQ), F32),
            pltpu.VMEM((2, TQ, TQ), BF16),
            pltpu.VMEM((2, TQ, TQ), F32),
            pltpu.VMEM((2, 2, 1, TQ), F32),
        ],
        compiler_params=_cparams(("parallel", "parallel", "arbitrary")),
        name="diffattn",
    )(slopes, lam_vecs, proj, proj, vt, subln_g)


def _outproj_kernel(h_ref, lru_ref, att_ref, wt_ref, wb_ref, g_ref, rw_ref, rb_ref,
                    hn_ref, xn_ref, gate_ref):
    h = (h_ref[...] + jnp.dot(lru_ref[...], wt_ref[...], preferred_element_type=F32)
         + jnp.dot(att_ref[...], wb_ref[...], preferred_element_type=F32))
    hn_ref[...] = h
    ms = jnp.mean(h * h, axis=-1, keepdims=True)
    xn = h * lax.rsqrt(ms + RMS_EPS) * g_ref[...]
    xn_ref[...] = xn.astype(xn_ref.dtype)

    logits = jnp.dot(xn, rw_ref[...], precision=lax.Precision.HIGHEST,
                     preferred_element_type=F32) + rb_ref[...]
    col = lax.broadcasted_iota(jnp.int32, logits.shape, 1)
    colf = col.astype(F32)
    big = float(ROUTER_COLS)
    neg = -jnp.inf
    is_g = col < N_GROUPS
    gl = jnp.where(is_g, logits, neg)
    gmax = jnp.max(gl, axis=1, keepdims=True)
    gsum = jnp.sum(jnp.where(is_g, jnp.exp(gl - gmax), 0.0), axis=1, keepdims=True)
    g_prob = 1.0 / gsum
    g_idx = jnp.min(jnp.where(gl == gmax, colf, big), axis=1, keepdims=True)
    ecol = col - N_GROUPS
    in_grp = (ecol >= 0) & (ecol < N_EXPERTS) & ((ecol // EXPERTS_PER_GROUP).astype(F32) == g_idx)
    el = jnp.where(in_grp, logits, neg)
    v1 = jnp.max(el, axis=1, keepdims=True)
    i1 = jnp.min(jnp.where(el == v1, colf, big), axis=1, keepdims=True)
    el2 = jnp.where(colf == i1, neg, el)
    v2 = jnp.max(el2, axis=1, keepdims=True)
    i2 = jnp.min(jnp.where(el2 == v2, colf, big), axis=1, keepdims=True)
    e2 = jnp.exp(v2 - v1)
    den = 1.0 + e2
    w_in = jnp.where(colf == i1, 1.0 / den, 0.0) + jnp.where(colf == i2, e2 / den, 0.0)
    gate_ref[...] = w_in * g_prob


def _outproj(h, lru, att, wt, wb, g, rw, rb, *, tm):
    n, d = h.shape
    c2 = lambda i: (0, 0)
    return pl.pallas_call(
        _outproj_kernel,
        out_shape=(jax.ShapeDtypeStruct((n, d), F32),
                   jax.ShapeDtypeStruct((n, d), BF16),
                   jax.ShapeDtypeStruct((n, ROUTER_COLS), F32)),
        grid=(n // tm,),
        in_specs=[
            pl.BlockSpec((tm, d), lambda i: (i, 0)),
            pl.BlockSpec((tm, LRU_WIDTH), lambda i: (i, 0)),
            pl.BlockSpec((tm, ATT_WIDTH), lambda i: (i, 0)),
            pl.BlockSpec((LRU_WIDTH, d), c2),
            pl.BlockSpec((ATT_WIDTH, d), c2),
            pl.BlockSpec((1, d), c2),
            pl.BlockSpec((d, ROUTER_COLS), c2),
            pl.BlockSpec((1, ROUTER_COLS), c2),
        ],
        out_specs=(pl.BlockSpec((tm, d), lambda i: (i, 0)),
                   pl.BlockSpec((tm, d), lambda i: (i, 0)),
                   pl.BlockSpec((tm, ROUTER_COLS), lambda i: (i, 0))),
        compiler_params=_cparams(("parallel",)),
        name="outproj_router",
    )(h, lru, att, wt, wb, g, rw, rb)


def _moe_kernel(x_ref, gate_ref, h_ref, wg_ref, wu_ref, wd_ref, o_ref, acc_ref):
    e = pl.program_id(1)

    @pl.when(e == 0)
    def _():
        acc_ref[...] = jnp.zeros_like(acc_ref)

    x = x_ref[...]
    hg = jnp.dot(x, wg_ref[...], preferred_element_type=F32)
    hu = jnp.dot(x, wu_ref[...], preferred_element_type=F32)
    he = (hg * jax.nn.sigmoid(hg)) * hu
    gate = gate_ref[...]
    col = lax.broadcasted_iota(jnp.int32, gate.shape, 1)
    ge = jnp.sum(jnp.where(col == e + N_GROUPS, gate, 0.0), axis=1, keepdims=True)
    y = jnp.dot(he.astype(BF16), wd_ref[...], preferred_element_type=F32)
    acc_ref[...] += ge * y

    @pl.when(e == N_EXPERTS - 1)
    def _():
        o_ref[...] = h_ref[...] + acc_ref[...]


def _moe(xn, gate, h, wg, wu, wd, *, tm):
    n, d = h.shape
    return pl.pallas_call(
        _moe_kernel,
        out_shape=jax.ShapeDtypeStruct((n, d), F32),
        grid=(n // tm, N_EXPERTS),
        in_specs=[
            pl.BlockSpec((tm, d), lambda i, e: (i, 0)),
            pl.BlockSpec((tm, ROUTER_COLS), lambda i, e: (i, 0)),
            pl.BlockSpec((tm, d), lambda i, e: (i, 0)),
            pl.BlockSpec((None, d, D_EXPERT), lambda i, e: (e, 0, 0)),
            pl.BlockSpec((None, d, D_EXPERT), lambda i, e: (e, 0, 0)),
            pl.BlockSpec((None, D_EXPERT, d), lambda i, e: (e, 0, 0)),
        ],
        out_specs=pl.BlockSpec((tm, d), lambda i, e: (i, 0)),
        scratch_shapes=[pltpu.VMEM((tm, d), F32)],
        compiler_params=_cparams(("parallel", "arbitrary")),
        name="moe_dense",
    )(xn, gate, h, wg, wu, wd)


def _final_norm_kernel(h_ref, g_ref, o_ref):
    x = h_ref[...]
    ms = jnp.mean(x * x, axis=-1, keepdims=True)
    o_ref[...] = x * lax.rsqrt(ms + RMS_EPS) * g_ref[...]


def _final_norm(h, g, *, batch, nt):
    d = h.shape[1]
    ntp = nt + 1
    return pl.pallas_call(
        _final_norm_kernel,
        out_shape=jax.ShapeDtypeStruct((batch, nt * TQ, d), F32),
        grid=(batch, nt),
        in_specs=[
            pl.BlockSpec((TQ, d), lambda b, i: (b * ntp + i, 0)),
            pl.BlockSpec((1, d), lambda b, i: (0, 0)),
        ],
        out_specs=pl.BlockSpec((None, TQ, d), lambda b, i: (b, i, 0)),
        compiler_params=_cparams(("parallel", "parallel")),
        name="final_norm",
    )(h, g)


def _row_tile(n, want):
    t = want
    while n % t:
        t //= 2
    return t


def kernel(x, meta_tokens, norm1_g, w_in, conv_w, conv_b, gate_a_w, gate_a_b, gate_x_w, gate_x_b, lru_a_param, lambda_q1, lambda_k1, lambda_q2, lambda_k2, subln_g, w_out, norm2_g, router_g_w, router_g_b, router_e_w, router_e_b, exp_w_gate, exp_w_up, exp_w_down, final_norm_g):
    batch, t, d = x.shape
    depth = w_in.shape[0]
    assert t % TQ == 0 and N_META <= LANES
    nt = t // TQ
    ntp = nt + 1
    tp = ntp * TQ
    n = batch * tp

    meta = jnp.broadcast_to(meta_tokens[None].astype(x.dtype), (batch, N_META, d))
    pad = jnp.zeros((batch, TQ - N_META, d), x.dtype)
    h = jnp.concatenate([x, meta, pad], axis=1).reshape(n, d)

    q_lo, q_hi = 2 * LRU_WIDTH, 2 * LRU_WIDTH + QK_WIDTH
    colsel = (jnp.arange(w_in.shape[2]) >= q_lo) & (jnp.arange(w_in.shape[2]) < q_hi)
    col_scale = jnp.where(colsel, ATT_QK_DIM ** -0.5 * LOG2E, 1.0).astype(F32)
    slopes = jnp.exp2(-8.0 * jnp.arange(1, ATT_HEADS + 1, dtype=F32) / ATT_HEADS) * LOG2E
    rpad = jnp.zeros((d, ROUTER_COLS - N_GROUPS - N_EXPERTS), F32)
    bpad = jnp.zeros((ROUTER_COLS - N_GROUPS - N_EXPERTS,), F32)

    vt_tail = jnp.concatenate(
        [jnp.ones((batch, ATT_HEADS, ntp, 1, TQ), BF16),
         jnp.zeros((batch, ATT_HEADS, ntp, VT_ROWS - ATT_V_DIM - 1, TQ), BF16)], axis=3)

    tm_proj = _row_tile(n, 1024)
    tm_tok = _row_tile(n, 512)

    for l in range(depth):
        lam_init = 0.8 - 0.6 * math.exp(-0.3 * l)
        w_in_l = (w_in[l] * col_scale).astype(BF16)
        proj = _inproj(h, norm1_g[l][None], w_in_l, tm=tm_proj, tn=512)

        lru = _lru(proj, conv_w[l], conv_b[l][None], gate_a_w[l].astype(BF16), gate_a_b[l][None],
                   gate_x_w[l].astype(BF16), gate_x_b[l][None], lru_a_param[l][None],
                   batch=batch, nt=nt)

        v = proj[:, q_hi + QK_WIDTH:]
        vt = v.reshape(batch, ntp, TQ, ATT_HEADS, ATT_V_DIM).transpose(0, 3, 1, 4, 2)
        vt = jnp.concatenate([vt, vt_tail], axis=3)
        lam_vecs = jnp.stack([lambda_q1[l], lambda_k1[l], lambda_q2[l], lambda_k2[l]]).astype(F32)
        att = _attention(proj, vt, slopes, lam_vecs, subln_g[l][:, None].astype(F32),
                         batch=batch, nt=nt, lam_init=lam_init)

        w_out_l = w_out[l].astype(BF16)
        rw = jnp.concatenate([router_g_w[l], router_e_w[l], rpad], axis=1)
        rb = jnp.concatenate([router_g_b[l], router_e_b[l], bpad])[None]
        h, xn, gate = _outproj(h, lru, att, w_out_l[:LRU_WIDTH], w_out_l[LRU_WIDTH:], norm2_g[l][None],
                               rw, rb, tm=tm_tok)

        h = _moe(xn, gate, h, exp_w_gate[l].astype(BF16), exp_w_up[l].astype(BF16),
                 exp_w_down[l].astype(BF16), tm=tm_tok)

    return _final_norm(h, final_norm_g[None], batch=batch, nt=nt)
```
